```python
import jax, jax.numpy as jnp
from jax import lax
import numpy as np

D_MODEL = 4096
BATCH = 1
SEQ = 16384
DEPTH = 1

CHUNK = 64
Q_BLOCK = 128
HEAD_DIM = 128
FOX_HEADS = 16
HGRN_HEADS = 16
FOX_WIDTH = FOX_HEADS * HEAD_DIM
HGRN_WIDTH = HGRN_HEADS * HEAD_DIM
N_GROUPS = 8
EXPERTS_PER_GROUP = 8
N_EXPERTS = N_GROUPS * EXPERTS_PER_GROUP
TOP_K_IN_GROUP = 2
D_EXPERT = 512
EXPERT_BLOCK = 128
PLE_DIM = 256
LN_EPS = 1e-5
RMS_EPS = 1e-6
DEEPNORM_ALPHA = (2 * DEPTH) ** 0.25
DEEPNORM_BETA = (8 * DEPTH) ** -0.25
IN_COLS = 3 * FOX_WIDTH + FOX_HEADS + 4 * HGRN_WIDTH + 2 * D_MODEL

kernel_name = 'fox_hgrn2_hier_moe_deepnorm_block'


def _layer_norm(x, g, b):
    xf = x.astype(jnp.float32)
    mu = jnp.mean(xf, axis=-1, keepdims=True)
    xc = xf - mu
    var = jnp.mean(xc * xc, axis=-1, keepdims=True)
    return (xc * lax.rsqrt(var + LN_EPS) * g.astype(jnp.float32) + b.astype(jnp.float32)).astype(x.dtype)


def _split_columns(a, sizes):
    out, start = [], 0
    for s in sizes:
        out.append(a[..., start:start + s])
        start += s
    return out


def _fox_attention(q, k, v, log_f):
    B, T, H, dh = q.shape
    nb = T // Q_BLOCK
    q = q.transpose(0, 2, 1, 3)
    k = k.transpose(0, 2, 1, 3)
    v = v.transpose(0, 2, 1, 3)
    c = jnp.cumsum(log_f, axis=1).transpose(0, 2, 1)
    scale = dh ** -0.5
    q_blocks = q.reshape(B, H, nb, Q_BLOCK, dh).transpose(2, 0, 1, 3, 4)
    c_blocks = c.reshape(B, H, nb, Q_BLOCK).transpose(2, 0, 1, 3)
    starts = jnp.arange(nb, dtype=jnp.int32) * Q_BLOCK
    k_pos = jnp.arange(T, dtype=jnp.int32)

    def block(args):
        q_blk, c_blk, start = args
        s = jnp.einsum('bhqd,bhkd->bhqk', q_blk, k).astype(jnp.float32) * scale
        s = s + c_blk[..., :, None] - c[:, :, None, :]
        q_pos = start + jnp.arange(Q_BLOCK, dtype=jnp.int32)
        mask = k_pos[None, :] <= q_pos[:, None]
        s = jnp.where(mask, s, -jnp.inf)
        prob = jax.nn.softmax(s, axis=-1)
        return jnp.einsum('bhqk,bhkd->bhqd', prob.astype(v.dtype), v)

    o = lax.map(block, (q_blocks, c_blocks, starts))
    return o.transpose(1, 0, 3, 2, 4).reshape(B, T, H * dh)


def _hgrn2_recurrence(q, k, v, log_f):
    B, T, H, K = q.shape
    V = v.shape[-1]
    nc = T // CHUNK

    def to_chunks(a):
        return a.reshape(B, nc, CHUNK, H, a.shape[-1]).transpose(1, 0, 3, 2, 4)

    causal = jnp.tril(jnp.ones((CHUNK, CHUNK), dtype=bool))

    def step(S, xs):
        qc, kc, vc, lc = xs
        b = jnp.cumsum(lc, axis=2)
        o_inter = jnp.einsum('bhlk,bhkv->bhlv', qc * jnp.exp(b), S)
        diff = b[:, :, :, None, :] - b[:, :, None, :, :]
        decay = jnp.exp(jnp.where(causal[:, :, None], diff, -jnp.inf))
        a = jnp.einsum('bhtk,bhsk,bhtsk->bhts', qc, kc, decay)
        o_intra = jnp.einsum('bhts,bhsv->bhtv', a, vc)
        b_last = b[:, :, -1:, :]
        S = jnp.exp(b_last[:, :, 0, :])[..., None] * S + jnp.einsum('bhsk,bhsv->bhkv', kc * jnp.exp(b_last - b), vc)
        return S, o_inter + o_intra

    S0 = jnp.zeros((B, H, K, V), jnp.float32)
    _, o = lax.scan(step, S0, (to_chunks(q), to_chunks(k), to_chunks(v), to_chunks(log_f)))
    return o.transpose(1, 0, 3, 2, 4).reshape(B, T, H, V)


def _hier_moe(h, w_rg, b_rg, w_re, b_re, w_gate, w_up, w_down):
    B, T, D = h.shape
    N = B * T
    xf = h.reshape(N, D)
    g_logits = (xf @ w_rg).astype(jnp.float32) + b_rg.astype(jnp.float32)
    g_prob = jax.nn.softmax(g_logits, axis=-1)
    g_sel = jnp.argmax(g_logits, axis=-1).astype(jnp.int32)
    g_w = jnp.take_along_axis(g_prob, g_sel[:, None], axis=1)[:, 0]
    e_logits = ((xf @ w_re).astype(jnp.float32) + b_re.astype(jnp.float32)).reshape(N, N_GROUPS, EXPERTS_PER_GROUP)
    e_logits = jnp.take_along_axis(e_logits, g_sel[:, None, None], axis=1)[:, 0]
    e_prob = jax.nn.softmax(e_logits, axis=-1)
    top_p, top_i = lax.top_k(e_prob, TOP_K_IN_GROUP)
    gate = top_p / jnp.sum(top_p, axis=-1, keepdims=True) * g_w[:, None]
    expert_id = g_sel[:, None] * EXPERTS_PER_GROUP + top_i.astype(jnp.int32)

    M = N * TOP_K_IN_GROUP
    n_blocks = -(-(M + N_EXPERTS * (EXPERT_BLOCK - 1)) // EXPERT_BLOCK)
    P = n_blocks * EXPERT_BLOCK
    flat_e = expert_id.reshape(M)
    flat_w = gate.reshape(M)
    flat_tok = jnp.repeat(jnp.arange(N, dtype=jnp.int32), TOP_K_IN_GROUP)
    order = jnp.argsort(flat_e)
    sorted_e = flat_e[order]
    counts = jnp.bincount(flat_e, length=N_EXPERTS).astype(jnp.int32)
    starts = jnp.cumsum(counts) - counts
    pcounts = (counts + EXPERT_BLOCK - 1) // EXPERT_BLOCK * EXPERT_BLOCK
    pends = jnp.cumsum(pcounts)
    pstarts = pends - pcounts
    rank = jnp.arange(M, dtype=jnp.int32) - starts[sorted_e]
    dest = pstarts[sorted_e] + rank
    slot_tok = jnp.full((P,), N, jnp.int32).at[dest].set(flat_tok[order])
    slot_w = jnp.zeros((P,), jnp.float32).at[dest].set(flat_w[order])
    block_start = jnp.arange(n_blocks, dtype=jnp.int32) * EXPERT_BLOCK
    block_exp = jnp.clip(jnp.searchsorted(pends, block_start, side='right'), 0, N_EXPERTS - 1).astype(jnp.int32)
    x_pad = jnp.concatenate([xf, jnp.zeros((1, D), xf.dtype)], axis=0)

    def body(y, blk):
        tok, e, wt = blk
        xb = x_pad[tok]
        hid = jax.nn.silu(xb @ w_gate[e]) * (xb @ w_up[e])
        out = (hid @ w_down[e]) * wt[:, None].astype(xb.dtype)
        return y.at[tok].add(out.astype(y.dtype)), None

    y0 = jnp.zeros((N + 1, D), xf.dtype)
    y, _ = lax.scan(body, y0, (slot_tok.reshape(n_blocks, EXPERT_BLOCK), block_exp, slot_w.reshape(n_blocks, EXPERT_BLOCK)))
    return y[:N].reshape(B, T, D)


def setup_inputs(seed: int = 0) -> dict:
    key = jax.random.key(seed)
    ks = jax.random.split(key, 23)
    f32 = jnp.float32
    L = DEPTH

    def nrm(k, shape, scale):
        return jax.random.normal(k, shape, f32) * scale

    return {
        'x': nrm(ks[0], (BATCH, SEQ, D_MODEL), 1.0),
        'p': nrm(ks[1], (L, BATCH, SEQ, PLE_DIM), 1.0),
        'w_in': nrm(ks[2], (L, D_MODEL, IN_COLS), D_MODEL ** -0.5),
        'b_fox_f': 3.0 + nrm(ks[3], (L, FOX_HEADS), 0.5),
        'hgrn_lb': nrm(ks[4], (L + 1, HGRN_WIDTH), 0.1),
        'hgrn_norm_g': 1.0 + nrm(ks[5], (L, HGRN_WIDTH), 0.01),
        'w_branch_a': nrm(ks[6], (L, FOX_WIDTH, D_MODEL), FOX_WIDTH ** -0.5),
        'w_branch_b': nrm(ks[7], (L, HGRN_WIDTH, D_MODEL), HGRN_WIDTH ** -0.5),
        'w_out': nrm(ks[8], (L, D_MODEL, D_MODEL), DEEPNORM_BETA * D_MODEL ** -0.5),
        'ln1_g': 1.0 + nrm(ks[9], (L, D_MODEL), 0.01),
        'ln1_b': nrm(ks[10], (L, D_MODEL), 0.01),
        'w_group_router': nrm(ks[11], (L, D_MODEL, N_GROUPS), D_MODEL ** -0.5),
        'b_group_router': nrm(ks[12], (L, N_GROUPS), 0.01),
        'w_expert_router': nrm(ks[13], (L, D_MODEL, N_EXPERTS), D_MODEL ** -0.5),
        'b_expert_router': nrm(ks[14], (L, N_EXPERTS), 0.01),
        'w_exp_gate': nrm(ks[15], (L, N_EXPERTS, D_MODEL, D_EXPERT), D_MODEL ** -0.5),
        'w_exp_up': nrm(ks[16], (L, N_EXPERTS, D_MODEL, D_EXPERT), D_MODEL ** -0.5),
        'w_exp_down': nrm(ks[17], (L, N_EXPERTS, D_EXPERT, D_MODEL), DEEPNORM_BETA * D_EXPERT ** -0.5),
        'ln2_g': 1.0 + nrm(ks[18], (L, D_MODEL), 0.01),
        'ln2_b': nrm(ks[19], (L, D_MODEL), 0.01),
        'w_ple_gate': nrm(ks[20], (L, D_MODEL, D_MODEL), D_MODEL ** -0.5),
        'b_ple_gate': nrm(ks[21], (L, D_MODEL), 0.01),
        'w_ple_proj': nrm(ks[22], (L, PLE_DIM, D_MODEL), PLE_DIM ** -0.5),
    }


def reference(x, p, w_in, b_fox_f, hgrn_lb, hgrn_norm_g, w_branch_a, w_branch_b, w_out, ln1_g, ln1_b,
              w_group_router, b_group_router, w_expert_router, b_expert_router, w_exp_gate, w_exp_up,
              w_exp_down, ln2_g, ln2_b, w_ple_gate, b_ple_gate, w_ple_proj):
    B, T, D = x.shape
    f32 = jnp.float32
    sizes = (FOX_WIDTH, FOX_WIDTH, FOX_WIDTH, FOX_HEADS,
             HGRN_WIDTH, HGRN_WIDTH, HGRN_WIDTH, HGRN_WIDTH, D_MODEL, D_MODEL)
    lower_bounds = jnp.cumsum(jax.nn.softmax(hgrn_lb.astype(f32), axis=0), axis=0)
    for layer in range(DEPTH):
        u = x @ w_in[layer]
        qa, ka, va, fa, qb, fb, ib, gb, gate_a, gate_b = _split_columns(u, sizes)

        log_fa = jax.nn.log_sigmoid(fa.astype(f32) + b_fox_f[layer].astype(f32))
        oa = _fox_attention(qa.reshape(B, T, FOX_HEADS, HEAD_DIM), ka.reshape(B, T, FOX_HEADS, HEAD_DIM),
                            va.reshape(B, T, FOX_HEADS, HEAD_DIM), log_fa)

        lb = lower_bounds[layer]
        f_b = lb + (1.0 - lb) * jax.nn.sigmoid(fb.astype(f32))
        shp = (B, T, HGRN_HEADS, HEAD_DIM)
        ob = _hgrn2_recurrence((qb.astype(f32) * HEAD_DIM ** -0.5).reshape(shp), (1.0 - f_b).reshape(shp),
                               ib.astype(f32).reshape(shp), jnp.log(f_b).reshape(shp))
        ob = ob * lax.rsqrt(jnp.mean(ob * ob, axis=-1, keepdims=True) + RMS_EPS)
        ob = ob * hgrn_norm_g[layer].astype(f32).reshape(HGRN_HEADS, HEAD_DIM)
        ob = (ob.reshape(B, T, HGRN_WIDTH) * jax.nn.silu(gb.astype(f32))).astype(x.dtype)

        y = jax.nn.sigmoid(gate_a) * (oa @ w_branch_a[layer]) + jax.nn.sigmoid(gate_b) * (ob @ w_branch_b[layer])
        x = _layer_norm(DEEPNORM_ALPHA * x + y @ w_out[layer], ln1_g[layer], ln1_b[layer])

        moe = _hier_moe(x, w_group_router[layer], b_group_router[layer], w_expert_router[layer],
                        b_expert_router[layer], w_exp_gate[layer], w_exp_up[layer], w_exp_down[layer])
        x = _layer_norm(DEEPNORM_ALPHA * x + moe, ln2_g[layer], ln2_b[layer])

        x = x + jax.nn.sigmoid(x @ w_ple_gate[layer] + b_ple_gate[layer]) * (p[layer] @ w_ple_proj[layer])
    return x
```

```python
import functools

import jax
import jax.numpy as jnp
from jax import lax
from jax.experimental import pallas as pl
from jax.experimental.pallas import tpu as pltpu

F32 = jnp.float32
BF16 = jnp.bfloat16

HEAD_DIM = 128
CHUNK = 64
TOP_K = 2
EXPERT_BLOCK = 128
LN_EPS = 1e-5
RMS_EPS = 1e-6
LANES = 128
VMEM_LIMIT = 56 * 1024 * 1024
NEG_INF = float("-inf")


def _cparams(sem):
    return pltpu.CompilerParams(dimension_semantics=sem, vmem_limit_bytes=VMEM_LIMIT)


def _split3(v):
    hi = v.astype(BF16)
    r1 = v - hi.astype(F32)
    mid = r1.astype(BF16)
    lo = (r1 - mid.astype(F32)).astype(BF16)
    return hi, mid, lo


def _tril_cumsum(v, n):
    row = lax.broadcasted_iota(jnp.int32, (n, n), 0)
    col = lax.broadcasted_iota(jnp.int32, (n, n), 1)
    tril = (row >= col).astype(BF16)
    hi, mid, lo = _split3(v)
    return (jnp.dot(tril, hi, preferred_element_type=F32)
            + jnp.dot(tril, mid, preferred_element_type=F32)
            + jnp.dot(tril, lo, preferred_element_type=F32))


def _sigmoid(z):
    return 1.0 / (1.0 + jnp.exp(-z))


def _layer_norm_rows(r, g, b):
    mu = jnp.mean(r, axis=-1, keepdims=True)
    xc = r - mu
    var = jnp.mean(xc * xc, axis=-1, keepdims=True)
    return xc * lax.rsqrt(var + LN_EPS) * g + b


def _proj_kernel(x_ref, w_ref, o_ref, *, epilogue):
    acc = jnp.dot(x_ref[...], w_ref[...], preferred_element_type=F32)
    o_ref[...] = epilogue(acc).astype(o_ref.dtype)


def _proj(xb, w, col_off, n_cols, out_dtype, epilogue, tm, tn):
    T, K = xb.shape
    return pl.pallas_call(
        functools.partial(_proj_kernel, epilogue=epilogue),
        grid=(T // tm, n_cols // tn),
        in_specs=[pl.BlockSpec((tm, K), lambda i, j: (i, 0)),
                  pl.BlockSpec((K, tn), lambda i, j: (0, j + col_off // tn))],
        out_specs=pl.BlockSpec((tm, tn), lambda i, j: (i, j)),
        out_shape=jax.ShapeDtypeStruct((T, n_cols), out_dtype),
        compiler_params=_cparams(("parallel", "arbitrary")),
        name="proj",
    )(xb, w)


def _fox_bias_kernel(x_ref, w_ref, b_ref, c_ref, carry_ref, *, tm, sub):
    @pl.when(pl.program_id(0) == 0)
    def _():
        carry_ref[...] = jnp.zeros_like(carry_ref)

    z = jnp.dot(x_ref[...], w_ref[...], preferred_element_type=F32) + b_ref[...]
    lf = jnp.minimum(z, 0.0) - jnp.log(1.0 + jnp.exp(-jnp.abs(z)))
    carry = carry_ref[...]
    for r in range(tm // sub):
        cs = _tril_cumsum(lf[r * sub:(r + 1) * sub], sub) + carry
        c_ref[r * sub:(r + 1) * sub, :] = cs
        carry = cs[sub - 1:sub, :]
    carry_ref[...] = carry


def _fox_bias(xb, w_f, b_f, tm):
    T, K = xb.shape
    sub = min(256, tm)
    return pl.pallas_call(
        functools.partial(_fox_bias_kernel, tm=tm, sub=sub),
        grid=(T // tm,),
        in_specs=[pl.BlockSpec((tm, K), lambda i: (i, 0)),
                  pl.BlockSpec((K, LANES), lambda i: (0, 0)),
                  pl.BlockSpec((1, LANES), lambda i: (0, 0))],
        out_specs=pl.BlockSpec((tm, LANES), lambda i: (i, 0)),
        out_shape=jax.ShapeDtypeStruct((T, LANES), F32),
        scratch_shapes=[pltpu.VMEM((1, LANES), F32)],
        compiler_params=_cparams(("arbitrary",)),
        name="fox_bias",
    )(xb, w_f, b_f)


def _fox_attn_kernel(q_ref, k_ref, v_ref, ck_ref, o_ref, *, tb):
    qi = pl.program_id(1)
    q = q_ref[...]

    def block(j, carry, masked):
        m, l, acc = carry
        off = pl.multiple_of(j * tb, tb)
        k = k_ref[pl.ds(off, tb), :]
        v = v_ref[pl.ds(off, tb), :]
        s = lax.dot_general(q, k, (((1,), (1,)), ((), ())), preferred_element_type=F32)
        s = s - ck_ref[0, j]
        if masked:
            row = lax.broadcasted_iota(jnp.int32, (tb, tb), 0)
            col = lax.broadcasted_iota(jnp.int32, (tb, tb), 1)
            s = jnp.where(col <= row, s, NEG_INF)
        m_new = jnp.maximum(m, jnp.max(s, axis=-1, keepdims=True))
        alpha = jnp.exp(m - m_new)
        p = jnp.exp(s - m_new)
        l = alpha * l + jnp.sum(p, axis=-1, keepdims=True)
        acc = alpha * acc + jnp.dot(p.astype(BF16), v, preferred_element_type=F32)
        return m_new, l, acc

    init = (jnp.full((tb, 1), NEG_INF, F32), jnp.zeros((tb, 1), F32),
            jnp.zeros((tb, HEAD_DIM), F32))
    carry = lax.fori_loop(0, qi, lambda j, c: block(j, c, False), init)
    m, l, acc = block(qi, carry, True)
    o_ref[...] = (acc / l).astype(o_ref.dtype)


def _fox_attn(q, kv, ck, n_heads, tb):
    T = q.shape[0]
    nb = T // tb
    return pl.pallas_call(
        functools.partial(_fox_attn_kernel, tb=tb),
        grid=(n_heads, nb),
        in_specs=[pl.BlockSpec((tb, HEAD_DIM), lambda h, i: (i, h)),
                  pl.BlockSpec((T, HEAD_DIM), lambda h, i: (0, h)),
                  pl.BlockSpec((T, HEAD_DIM), lambda h, i: (0, n_heads + h)),
                  pl.BlockSpec((1, nb, 1, tb), lambda h, i: (h, 0, 0, 0))],
        out_specs=pl.BlockSpec((tb, HEAD_DIM), lambda h, i: (i, h)),
        out_shape=jax.ShapeDtypeStruct((T, n_heads * HEAD_DIM), BF16),
        compiler_params=_cparams(("parallel", "arbitrary")),
        name="fox_attn",
    )(q, kv, kv, ck)


def _hgrn_kernel(q_ref, f_ref, i_ref, g_ref, lb_ref, gain_ref, o_ref, st_ref, *, rows, hg):
    @pl.when(pl.program_id(1) == 0)
    def _():
        st_ref[...] = jnp.zeros_like(st_ref)

    lb = lb_ref[...]
    f = lb + (1.0 - lb) * _sigmoid(f_ref[...])
    kk = 1.0 - f
    c = _tril_cumsum(jnp.log(f), rows)
    gain = gain_ref[...]
    r64 = lax.broadcasted_iota(jnp.int32, (CHUNK, CHUNK), 0)
    c64 = lax.broadcasted_iota(jnp.int32, (CHUNK, CHUNK), 1)
    causal = c64 <= r64

    for ci in range(rows // CHUNK):
        r0 = ci * CHUNK
        b = c[r0:r0 + CHUNK]
        if ci > 0:
            b = b - c[r0 - 1:r0]
        b_last = b[CHUNK - 1:CHUNK]
        eb = jnp.exp(b)
        enb = jnp.exp(-b)
        ebl = jnp.exp(b_last - b)
        dl = jnp.exp(b_last)
        qc = q_ref[r0:r0 + CHUNK, :].astype(F32)
        kc = kk[r0:r0 + CHUNK]
        qp_all = (qc * eb).astype(BF16)
        kp_all = (kc * enb).astype(BF16)
        kl_all = (kc * ebl).astype(BF16)
        vc = i_ref[r0:r0 + CHUNK, :]
        gc = g_ref[r0:r0 + CHUNK, :].astype(F32)
        for h in range(hg):
            ls = slice(h * HEAD_DIM, (h + 1) * HEAD_DIM)
            qp, kp, kl, vh = qp_all[:, ls], kp_all[:, ls], kl_all[:, ls], vc[:, ls]
            a = lax.dot_general(qp, kp, (((1,), (1,)), ((), ())), preferred_element_type=F32)
            a = jnp.where(causal, a, 0.0)
            st = st_ref[h]
            o = jnp.dot(a.astype(BF16), vh, preferred_element_type=F32)
            o = o + lax.dot_general(qp, st.astype(BF16), (((1,), (1,)), ((), ())),
                                    preferred_element_type=F32)
            vt = vh.astype(F32).T.astype(BF16)
            st_ref[h] = st * dl[:, ls] + jnp.dot(vt, kl, preferred_element_type=F32)
            o = o * lax.rsqrt(jnp.mean(o * o, axis=-1, keepdims=True) + RMS_EPS)
            gh = gc[:, ls]
            o = o * gain[:, ls] * (gh * _sigmoid(gh))
            o_ref[r0:r0 + CHUNK, ls] = o.astype(o_ref.dtype)


def _hgrn(qb, fb, ib, gb, lb, gain, rows, hg):
    T, W = qb.shape
    wb = hg * HEAD_DIM
    big = pl.BlockSpec((rows, wb), lambda h, i: (i, h))
    vec = pl.BlockSpec((1, wb), lambda h, i: (0, h))
    return pl.pallas_call(
        functools.partial(_hgrn_kernel, rows=rows, hg=hg),
        grid=(W // wb, T // rows),
        in_specs=[big, big, big, big, vec, vec],
        out_specs=big,
        out_shape=jax.ShapeDtypeStruct((T, W), BF16),
        scratch_shapes=[pltpu.VMEM((hg, HEAD_DIM, HEAD_DIM), F32)],
        compiler_params=_cparams(("parallel", "arbitrary")),
        name="hgrn",
    )(qb, fb, ib, gb, lb, gain)


def _merge_kernel(oa_ref, ob_ref, wa_ref, wb_ref, ga_ref, gb_ref, y_ref):
    a = jnp.dot(oa_ref[...], wa_ref[...], preferred_element_type=F32)
    b = jnp.dot(ob_ref[...], wb_ref[...], preferred_element_type=F32)
    y_ref[...] = (ga_ref[...].astype(F32) * a + gb_ref[...].astype(F32) * b).astype(y_ref.dtype)


def _merge(oa, ob, wa, wb, gates, tm, tn):
    T, KA = oa.shape
    KB = ob.shape[1]
    D = wa.shape[1]
    nd = D // tn
    return pl.pallas_call(
        _merge_kernel,
        grid=(T // tm, nd),
        in_specs=[pl.BlockSpec((tm, KA), lambda i, j: (i, 0)),
                  pl.BlockSpec((tm, KB), lambda i, j: (i, 0)),
                  pl.BlockSpec((KA, tn), lambda i, j: (0, j)),
                  pl.BlockSpec((KB, tn), lambda i, j: (0, j)),
                  pl.BlockSpec((tm, tn), lambda i, j: (i, j)),
                  pl.BlockSpec((tm, tn), lambda i, j: (i, j + nd))],
        out_specs=pl.BlockSpec((tm, tn), lambda i, j: (i, j)),
        out_shape=jax.ShapeDtypeStruct((T, D), BF16),
        compiler_params=_cparams(("parallel", "arbitrary")),
        name="merge",
    )(oa, ob, wa, wb, gates, gates)


def _resid_mm_kernel(y_ref, w_ref, x_ref, r_ref, *, alpha):
    z = jnp.dot(y_ref[...], w_ref[...], preferred_element_type=F32)
    r_ref[...] = alpha * x_ref[...] + z


def _resid_mm(y, w, x, alpha, tm, tn):
    T, K = y.shape
    D = w.shape[1]
    return pl.pallas_call(
        functools.partial(_resid_mm_kernel, alpha=alpha),
        grid=(T // tm, D // tn),
        in_specs=[pl.BlockSpec((tm, K), lambda i, j: (i, 0)),
                  pl.BlockSpec((K, tn), lambda i, j: (0, j)),
                  pl.BlockSpec((tm, tn), lambda i, j: (i, j))],
        out_specs=pl.BlockSpec((tm, tn), lambda i, j: (i, j)),
        out_shape=jax.ShapeDtypeStruct((T, D), F32),
        compiler_params=_cparams(("parallel", "arbitrary")),
        name="resid_mm",
    )(y, w, x)


def _ln_route_kernel(r_ref, g_ref, b_ref, wr_ref, br_ref, x1_ref, x1b_ref, lg_ref):
    x1 = _layer_norm_rows(r_ref[...], g_ref[...], b_ref[...])
    x1_ref[...] = x1
    x1b = x1.astype(BF16)
    x1b_ref[...] = x1b
    lg_ref[...] = jnp.dot(x1b, wr_ref[...], preferred_element_type=F32) + br_ref[...]


def _ln_route(r, g, b, wr, br, tm):
    T, D = r.shape
    row = pl.BlockSpec((tm, D), lambda i: (i, 0))
    vec = pl.BlockSpec((1, D), lambda i: (0, 0))
    return pl.pallas_call(
        _ln_route_kernel,
        grid=(T // tm,),
        in_specs=[row, vec, vec,
                  pl.BlockSpec((D, LANES), lambda i: (0, 0)),
                  pl.BlockSpec((1, LANES), lambda i: (0, 0))],
        out_specs=[row, row, pl.BlockSpec((tm, LANES), lambda i: (i, 0))],
        out_shape=[jax.ShapeDtypeStruct((T, D), F32),
                   jax.ShapeDtypeStruct((T, D), BF16),
                   jax.ShapeDtypeStruct((T, LANES), F32)],
        compiler_params=_cparams(("parallel",)),
        name="ln_route",
    )(r, g, b, wr, br)


def _route_kernel(lg_ref, ids_ref, wts_ref, cnt_ref, carry_ref, *, tm, n_groups, per_group):
    @pl.when(pl.program_id(0) == 0)
    def _():
        carry_ref[...] = jnp.zeros_like(carry_ref)

    lg = lg_ref[...]
    lane = lax.broadcasted_iota(jnp.int32, (tm, LANES), 1)

    def first_argmax(v):
        mx = jnp.max(v, axis=-1, keepdims=True)
        idx = jnp.min(jnp.where(v == mx, lane, LANES), axis=-1, keepdims=True)
        return mx, idx

    gl = jnp.where(lane < n_groups, lg, NEG_INF)
    gmax, gsel = first_argmax(gl)
    g_w = 1.0 / jnp.sum(jnp.exp(gl - gmax), axis=-1, keepdims=True)

    e_lane = lane - n_groups
    in_group = (e_lane >= gsel * per_group) & (e_lane < (gsel + 1) * per_group)
    el = jnp.where(in_group, lg, NEG_INF)
    m1, i1 = first_argmax(el)
    m2, i2 = first_argmax(jnp.where(lane == i1, NEG_INF, el))
    r21 = jnp.exp(m2 - m1)
    w1 = g_w / (1.0 + r21)
    w2 = g_w * r21 / (1.0 + r21)

    onehot = ((lane == i1) | (lane == i2)).astype(BF16)
    row = lax.broadcasted_iota(jnp.int32, (tm, tm), 0)
    col = lax.broadcasted_iota(jnp.int32, (tm, tm), 1)
    strict = (col < row).astype(BF16)
    prefix = jnp.dot(strict, onehot, preferred_element_type=F32) + carry_ref[...]
    rank1 = jnp.sum(jnp.where(lane == i1, prefix, 0.0), axis=-1, keepdims=True)
    rank2 = jnp.sum(jnp.where(lane == i2, prefix, 0.0), axis=-1, keepdims=True)
    ones = jnp.ones((8, tm), BF16)
    carry = carry_ref[...] + jnp.dot(ones, onehot, preferred_element_type=F32)[0:1]
    carry_ref[...] = carry
    cnt_ref[...] = carry

    e1 = i1 - n_groups
    e2 = i2 - n_groups
    ids = jnp.where(lane == 0, e1, jnp.where(lane == 1, e2, 0))
    ids = jnp.where(lane == 2, rank1.astype(jnp.int32), ids)
    ids = jnp.where(lane == 3, rank2.astype(jnp.int32), ids)
    ids_ref[...] = ids
    wts_ref[...] = jnp.where(lane == 0, w1, jnp.where(lane == 1, w2, 0.0))


def _route(logits, n_groups, per_group, tm):
    T = logits.shape[0]
    blk = pl.BlockSpec((tm, LANES), lambda i: (i, 0))
    return pl.pallas_call(
        functools.partial(_route_kernel, tm=tm, n_groups=n_groups, per_group=per_group),
        grid=(T // tm,),
        in_specs=[blk],
        out_specs=[blk, blk, pl.BlockSpec((1, LANES), lambda i: (0, 0))],
        out_shape=[jax.ShapeDtypeStruct((T, LANES), jnp.int32),
                   jax.ShapeDtypeStruct((T, LANES), F32),
                   jax.ShapeDtypeStruct((1, LANES), F32)],
        scratch_shapes=[pltpu.VMEM((1, LANES), F32)],
        compiler_params=_cparams(("arbitrary",)),
        name="route",
    )(logits)


def _gather_kernel(idx_ref, src_ref, out_ref, sem, *, rows):
    base = pl.program_id(0) * rows

    def row_copy(src_row, dst_row):
        return pltpu.make_async_copy(src_ref.at[pl.ds(src_row, 1)],
                                     out_ref.at[pl.ds(dst_row, 1)], sem)

    def issue(r, _):
        row_copy(idx_ref[0, 0, r], base + r).start()
        return 0

    def drain(r, _):
        row_copy(0, base + r).wait()
        return 0

    lax.fori_loop(0, rows, issue, 0)
    lax.fori_loop(0, rows, drain, 0)


def _gather_rows(src, idx, rows):
    n = idx.shape[0]
    D = src.shape[1]
    idx3 = idx.reshape(n // rows, 1, rows)
    return pl.pallas_call(
        functools.partial(_gather_kernel, rows=rows),
        grid=(n // rows,),
        in_specs=[pl.BlockSpec((1, 1, rows), lambda i: (i, 0, 0), memory_space=pltpu.SMEM),
                  pl.BlockSpec(memory_space=pl.ANY)],
        out_specs=pl.BlockSpec(memory_space=pl.ANY),
        out_shape=jax.ShapeDtypeStruct((n, D), src.dtype),
        scratch_shapes=[pltpu.SemaphoreType.DMA(())],
        compiler_params=_cparams(("arbitrary",)),
        name="gather",
    )(idx3, src)


def _experts_kernel(be_ref, xs_ref, wg_ref, wu_ref, wd_ref, o_ref):
    del be_ref
    xb = xs_ref[...].astype(BF16)
    g = jnp.dot(xb, wg_ref[...], preferred_element_type=F32)
    u = jnp.dot(xb, wu_ref[...], preferred_element_type=F32)
    hid = (g * _sigmoid(g) * u).astype(BF16)
    o_ref[...] = jnp.dot(hid, wd_ref[...], preferred_element_type=F32)


def _experts(block_exp, xs, wg, wu, wd):
    P, D = xs.shape
    DE = wg.shape[2]
    grid_spec = pltpu.PrefetchScalarGridSpec(
        num_scalar_prefetch=1,
        grid=(P // EXPERT_BLOCK,),
        in_specs=[pl.BlockSpec((EXPERT_BLOCK, D), lambda i, be: (i, 0)),
                  pl.BlockSpec((None, D, DE), lambda i, be: (be[i], 0, 0)),
                  pl.BlockSpec((None, D, DE), lambda i, be: (be[i], 0, 0)),
                  pl.BlockSpec((None, DE, D), lambda i, be: (be[i], 0, 0))],
        out_specs=pl.BlockSpec((EXPERT_BLOCK, D), lambda i, be: (i, 0)),
    )
    return pl.pallas_call(
        _experts_kernel,
        grid_spec=grid_spec,
        out_shape=jax.ShapeDtypeStruct((P, D), F32),
        compiler_params=_cparams(("arbitrary",)),
        name="experts",
    )(block_exp, xs, wg, wu, wd)


def _ln2_kernel(x1_ref, m0_ref, m1_ref, wts_ref, g_ref, b_ref, x2_ref, x2b_ref, *, alpha):
    w = wts_ref[...]
    r = alpha * x1_ref[...] + w[:, 0:1] * m0_ref[...] + w[:, 1:2] * m1_ref[...]
    x2 = _layer_norm_rows(r, g_ref[...], b_ref[...])
    x2_ref[...] = x2
    x2b_ref[...] = x2.astype(BF16)


def _ln2(x1, m0, m1, wts, g, b, alpha, tm):
    T, D = x1.shape
    row = pl.BlockSpec((tm, D), lambda i: (i, 0))
    vec = pl.BlockSpec((1, D), lambda i: (0, 0))
    return pl.pallas_call(
        functools.partial(_ln2_kernel, alpha=alpha),
        grid=(T // tm,),
        in_specs=[row, row, row, pl.BlockSpec((tm, LANES), lambda i: (i, 0)), vec, vec],
        out_specs=[row, row],
        out_shape=[jax.ShapeDtypeStruct((T, D), F32), jax.ShapeDtypeStruct((T, D), BF16)],
        compiler_params=_cparams(("parallel",)),
        name="ln2",
    )(x1, m0, m1, wts, g, b)


def _ple_kernel(x2b_ref, wpg_ref, bpg_ref, p_ref, wpe_ref, x2_ref, o_ref):
    z = jnp.dot(x2b_ref[...], wpg_ref[...], preferred_element_type=F32) + bpg_ref[...]
    e = jnp.dot(p_ref[...], wpe_ref[...], preferred_element_type=F32)
    o_ref[...] = x2_ref[...] + _sigmoid(z) * e


def _ple(x2b, wpg, bpg, pb, wpe, x2, tm, tn):
    T, D = x2b.shape
    PL = pb.shape[1]
    return pl.pallas_call(
        _ple_kernel,
        grid=(T // tm, D // tn),
        in_specs=[pl.BlockSpec((tm, D), lambda i, j: (i, 0)),
                  pl.BlockSpec((D, tn), lambda i, j: (0, j)),
                  pl.BlockSpec((1, tn), lambda i, j: (0, j)),
                  pl.BlockSpec((tm, PL), lambda i, j: (i, 0)),
                  pl.BlockSpec((PL, tn), lambda i, j: (0, j)),
                  pl.BlockSpec((tm, tn), lambda i, j: (i, j))],
        out_specs=pl.BlockSpec((tm, tn), lambda i, j: (i, j)),
        out_shape=jax.ShapeDtypeStruct((T, D), F32),
        compiler_params=_cparams(("parallel", "arbitrary")),
        name="ple",
    )(x2b, wpg, bpg, pb, wpe, x2)


def _tile(n, pref):
    return pref if n % pref == 0 else n


def _layer(x, p, w_in, b_fox_f, lb, hgrn_gain, w_a, w_b, w_o, ln1_g, ln1_b, w_rg, b_rg, w_re, b_re,
           w_eg, w_eu, w_ed, ln2_g, ln2_b, w_pg, b_pg, w_pe, alpha):
    T, D = x.shape
    n_fox = b_fox_f.shape[0]
    fw = n_fox * HEAD_DIM
    hw = lb.shape[0]
    n_groups = w_rg.shape[1]
    n_exp = w_re.shape[1]
    per_group = n_exp // n_groups
    assert n_groups + n_exp <= LANES

    tm = _tile(T, 1024)
    tn = _tile(D, 512)
    scale = HEAD_DIM ** -0.5

    xb = x.astype(BF16)
    f0 = 3 * fw
    h0 = f0 + n_fox
    w_main = jnp.concatenate([w_in[:, :f0], w_in[:, h0:]], axis=1).astype(BF16)
    w_f = jnp.zeros((D, LANES), BF16).at[:, :n_fox].set(w_in[:, f0:h0].astype(BF16))
    b_f = jnp.zeros((1, LANES), F32).at[0, :n_fox].set(b_fox_f)

    tnp = _tile(fw, 512)
    ident = lambda a: a
    q = _proj(xb, w_main, 0, fw, BF16, lambda a: a * scale, tm, tnp)
    kv = _proj(xb, w_main, fw, 2 * fw, BF16, ident, tm, tnp)
    o_h = 3 * fw
    tnh = _tile(hw, 512)
    qh = _proj(xb, w_main, o_h, hw, BF16, lambda a: a * scale, tm, tnh)
    fh = _proj(xb, w_main, o_h + hw, hw, F32, ident, tm, tnh)
    ih = _proj(xb, w_main, o_h + 2 * hw, hw, BF16, ident, tm, tnh)
    gh = _proj(xb, w_main, o_h + 3 * hw, hw, BF16, ident, tm, tnh)
    gates = _proj(xb, w_main, o_h + 4 * hw, 2 * D, BF16, _sigmoid, tm, tn)

    c = _fox_bias(xb, w_f, b_f, _tile(T, 512))
    tb = _tile(T, 512)
    ck = c[:, :n_fox].T.reshape(n_fox, T // tb, 1, tb)
    oa = _fox_attn(q, kv, ck, n_fox, tb)

    rows = _tile(T, 256)
    ob = _hgrn(qh, fh, ih, gh, lb.reshape(1, hw), hgrn_gain.reshape(1, hw), rows, 2)

    y = _merge(oa, ob, w_a.astype(BF16), w_b.astype(BF16), gates, tm, tn)
    r1 = _resid_mm(y, w_o.astype(BF16), x, alpha, tm, tn)
    w_r = jnp.zeros((D, LANES), BF16).at[:, :n_groups + n_exp].set(
        jnp.concatenate([w_rg, w_re], axis=1).astype(BF16))
    b_r = jnp.zeros((1, LANES), F32).at[0, :n_groups + n_exp].set(jnp.concatenate([b_rg, b_re]))
    x1, x1b, logits = _ln_route(r1, ln1_g.reshape(1, D), ln1_b.reshape(1, D), w_r, b_r, _tile(T, 256))
    del x1b

    ids, wts, counts = _route(logits, n_groups, per_group, _tile(T, 512))
    expert_id = ids[:, 0:2]
    rank = ids[:, 2:4]
    cnt = counts[0, n_groups:n_groups + n_exp].astype(jnp.int32)
    pcnt = (cnt + EXPERT_BLOCK - 1) // EXPERT_BLOCK * EXPERT_BLOCK
    pends = jnp.cumsum(pcnt)
    pstarts = pends - pcnt
    M = T * TOP_K
    n_blocks = -(-(M + n_exp * (EXPERT_BLOCK - 1)) // EXPERT_BLOCK)
    P = n_blocks * EXPERT_BLOCK
    dest = pstarts[expert_id] + rank
    tok = jnp.broadcast_to(jnp.arange(T, dtype=jnp.int32)[:, None], (T, TOP_K))
    slot_tok = jnp.zeros((P,), jnp.int32).at[dest.reshape(M)].set(tok.reshape(M))
    block_start = jnp.arange(n_blocks, dtype=jnp.int32) * EXPERT_BLOCK
    block_exp = jnp.clip(jnp.searchsorted(pends, block_start, side='right'), 0, n_exp - 1).astype(jnp.int32)

    grows = _tile(P, 512)
    xs = _gather_rows(x1, slot_tok, grows)
    outs = _experts(block_exp, xs, w_eg.astype(BF16), w_eu.astype(BF16), w_ed.astype(BF16))
    trows = _tile(T, 512)
    m0 = _gather_rows(outs, dest[:, 0], trows)
    m1 = _gather_rows(outs, dest[:, 1], trows)

    x2, x2b = _ln2(x1, m0, m1, wts, ln2_g.reshape(1, D), ln2_b.reshape(1, D), alpha, _tile(T, 256))
    return _ple(x2b, w_pg.astype(BF16), b_pg.reshape(1, D), p.astype(BF16), w_pe.astype(BF16), x2, tm, tn)


def kernel(x, p, w_in, b_fox_f, hgrn_lb, hgrn_norm_g, w_branch_a, w_branch_b, w_out, ln1_g, ln1_b, w_group_router, b_group_router, w_expert_router, b_expert_router, w_exp_gate, w_exp_up, w_exp_down, ln2_g, ln2_b, w_ple_gate, b_ple_gate, w_ple_proj):
    B, T, D = x.shape
    depth = w_in.shape[0]
    alpha = (2 * depth) ** 0.25
    lower_bounds = jnp.cumsum(jax.nn.softmax(hgrn_lb.astype(F32), axis=0), axis=0)
    outs = []
    for bi in range(B):
        xc = x[bi]
        for l in range(depth):
            xc = _layer(xc, p[l, bi], w_in[l], b_fox_f[l], lower_bounds[l], hgrn_norm_g[l],
                        w_branch_a[l], w_branch_b[l], w_out[l], ln1_g[l], ln1_b[l],
                        w_group_router[l], b_group_router[l], w_expert_router[l], b_expert_router[l],
                        w_exp_gate[l], w_exp_up[l], w_exp_down[l], ln2_g[l], ln2_b[l],
                        w_ple_gate[l], b_ple_gate[l], w_ple_proj[l], alpha)
        outs.append(xc)
    return jnp.stack(outs, axis=0)
```

```python
import functools

import jax
import jax.numpy as jnp
from jax import lax
from jax.experimental import pallas as pl
from jax.experimental.pallas import tpu as pltpu

F32 = jnp.float32
BF16 = jnp.bfloat16

HEAD_DIM = 128
CHUNK = 64
TOP_K = 2
EXPERT_BLOCK = 128
DUMP_ROWS = 2 * EXPERT_BLOCK
LOG2E = 1.4426950408889634
LN_EPS = 1e-5
RMS_EPS = 1e-6
LANES = 128
VMEM_LIMIT = 56 * 1024 * 1024
NEG_INF = float("-inf")


def _cparams(sem):
    return pltpu.CompilerParams(dimension_semantics=sem, vmem_limit_bytes=VMEM_LIMIT)


def _split3(v):
    hi = v.astype(BF16)
    r1 = v - hi.astype(F32)
    mid = r1.astype(BF16)
    lo = (r1 - mid.astype(F32)).astype(BF16)
    return hi, mid, lo


def _tril_cumsum(v, n):
    row = lax.broadcasted_iota(jnp.int32, (n, n), 0)
    col = lax.broadcasted_iota(jnp.int32, (n, n), 1)
    tril = (row >= col).astype(BF16)
    hi, mid, lo = _split3(v)
    return (jnp.dot(tril, hi, preferred_element_type=F32)
            + jnp.dot(tril, mid, preferred_element_type=F32)
            + jnp.dot(tril, lo, preferred_element_type=F32))


def _sigmoid(z):
    return 1.0 / (1.0 + jnp.exp(-z))


def _layer_norm_rows(r, g, b):
    mu = jnp.mean(r, axis=-1, keepdims=True)
    xc = r - mu
    var = jnp.mean(xc * xc, axis=-1, keepdims=True)
    return xc * lax.rsqrt(var + LN_EPS) * g + b


def _proj_kernel(x_ref, w_ref, o_ref, *, epilogue):
    acc = jnp.dot(x_ref[...], w_ref[...], preferred_element_type=F32)
    o_ref[...] = epilogue(acc).astype(o_ref.dtype)


def _proj(xb, w, col_off, n_cols, out_dtype, epilogue, tm, tn):
    T, K = xb.shape
    return pl.pallas_call(
        functools.partial(_proj_kernel, epilogue=epilogue),
        grid=(T // tm, n_cols // tn),
        in_specs=[pl.BlockSpec((tm, K), lambda i, j: (i, 0)),
                  pl.BlockSpec((K, tn), lambda i, j: (0, j + col_off // tn))],
        out_specs=pl.BlockSpec((tm, tn), lambda i, j: (i, j)),
        out_shape=jax.ShapeDtypeStruct((T, n_cols), out_dtype),
        compiler_params=_cparams(("parallel", "arbitrary")),
        name="proj",
    )(xb, w)


def _fox_bias_kernel(x_ref, w_ref, b_ref, c_ref, carry_ref, *, tm, sub):
    @pl.when(pl.program_id(0) == 0)
    def _():
        carry_ref[...] = jnp.zeros_like(carry_ref)

    z = jnp.dot(x_ref[...], w_ref[...], preferred_element_type=F32) + b_ref[...]
    lf = jnp.minimum(z, 0.0) - jnp.log(1.0 + jnp.exp(-jnp.abs(z)))
    carry = carry_ref[...]
    for r in range(tm // sub):
        cs = _tril_cumsum(lf[r * sub:(r + 1) * sub], sub) + carry
        c_ref[r * sub:(r + 1) * sub, :] = cs * LOG2E
        carry = cs[sub - 1:sub, :]
    carry_ref[...] = carry


def _fox_bias(xb, w_f, b_f, tm):
    T, K = xb.shape
    sub = min(256, tm)
    return pl.pallas_call(
        functools.partial(_fox_bias_kernel, tm=tm, sub=sub),
        grid=(T // tm,),
        in_specs=[pl.BlockSpec((tm, K), lambda i: (i, 0)),
                  pl.BlockSpec((K, LANES), lambda i: (0, 0)),
                  pl.BlockSpec((1, LANES), lambda i: (0, 0))],
        out_specs=pl.BlockSpec((tm, LANES), lambda i: (i, 0)),
        out_shape=jax.ShapeDtypeStruct((T, LANES), F32),
        scratch_shapes=[pltpu.VMEM((1, LANES), F32)],
        compiler_params=_cparams(("arbitrary",)),
        name="fox_bias",
    )(xb, w_f, b_f)


def _fox_attn_kernel(q_ref, k_ref, v_ref, ck_ref, o_ref, *, tq, tk):
    qi = pl.program_id(1)
    per_trip = tq // tk
    q = q_ref[...]
    ones = jnp.ones((tk, HEAD_DIM), BF16)

    def block(j, u, carry, masked):
        m, l, acc = carry
        off = pl.multiple_of(j * tk, tk)
        k = k_ref[pl.ds(off, tk), :]
        va = jnp.concatenate([v_ref[pl.ds(off, tk), :], ones], axis=1)
        s = lax.dot_general(q, k, (((1,), (1,)), ((), ())), preferred_element_type=F32)
        s = s - ck_ref[0, j]
        if masked:
            row = lax.broadcasted_iota(jnp.int32, (tq, tk), 0)
            col = lax.broadcasted_iota(jnp.int32, (tq, tk), 1) + u * tk
            s = jnp.where(col <= row, s, NEG_INF)
        m_new = jnp.maximum(m, jnp.max(s, axis=-1, keepdims=True))
        alpha = jnp.exp2(m - m_new)
        p = jnp.exp2(s - m_new).astype(BF16)
        pv = jnp.dot(p, va, preferred_element_type=F32)
        l = alpha * l + pv[:, HEAD_DIM:HEAD_DIM + 1]
        acc = alpha * acc + pv[:, :HEAD_DIM]
        return m_new, l, acc

    def trip(t, carry, masked):
        for u in range(per_trip):
            carry = block(t * per_trip + u, u, carry, masked)
        return carry

    init = (jnp.full((tq, 1), NEG_INF, F32), jnp.zeros((tq, 1), F32),
            jnp.zeros((tq, HEAD_DIM), F32))
    carry = lax.fori_loop(0, qi, lambda t, c: trip(t, c, False), init)
    m, l, acc = trip(qi, carry, True)
    o_ref[...] = (acc / l).astype(o_ref.dtype)


def _fox_attn(q, kv, ck, n_heads, tq, tk):
    T = q.shape[0]
    nk = T // tk
    return pl.pallas_call(
        functools.partial(_fox_attn_kernel, tq=tq, tk=tk),
        grid=(n_heads, T // tq),
        in_specs=[pl.BlockSpec((tq, HEAD_DIM), lambda h, i: (i, h)),
                  pl.BlockSpec((T, HEAD_DIM), lambda h, i: (0, h)),
                  pl.BlockSpec((T, HEAD_DIM), lambda h, i: (0, n_heads + h)),
                  pl.BlockSpec((1, nk, 1, tk), lambda h, i: (h, 0, 0, 0))],
        out_specs=pl.BlockSpec((tq, HEAD_DIM), lambda h, i: (i, h)),
        out_shape=jax.ShapeDtypeStruct((T, n_heads * HEAD_DIM), BF16),
        compiler_params=_cparams(("parallel", "arbitrary")),
        name="fox_attn",
    )(q, kv, kv, ck)


def _hgrn_kernel(q_ref, f_ref, i_ref, g_ref, lb_ref, gain_ref, o_ref, st_ref, *, rows, hg):
    @pl.when(pl.program_id(1) == 0)
    def _():
        st_ref[...] = jnp.zeros_like(st_ref)

    lb = lb_ref[...]
    f = lb + (1.0 - lb) * _sigmoid(f_ref[...])
    kk = 1.0 - f
    c = _tril_cumsum(jnp.log(f), rows)
    gain = gain_ref[...]
    r64 = lax.broadcasted_iota(jnp.int32, (CHUNK, CHUNK), 0)
    c64 = lax.broadcasted_iota(jnp.int32, (CHUNK, CHUNK), 1)
    causal = c64 <= r64

    for ci in range(rows // CHUNK):
        r0 = ci * CHUNK
        b = c[r0:r0 + CHUNK]
        if ci > 0:
            b = b - c[r0 - 1:r0]
        b_last = b[CHUNK - 1:CHUNK]
        eb = jnp.exp(b)
        enb = jnp.exp(-b)
        ebl = jnp.exp(b_last - b)
        dl = jnp.exp(b_last)
        qc = q_ref[r0:r0 + CHUNK, :].astype(F32)
        kc = kk[r0:r0 + CHUNK]
        qp_all = (qc * eb).astype(BF16)
        kp_all = (kc * enb).astype(BF16)
        kl_all = (kc * ebl).astype(BF16)
        vc = i_ref[r0:r0 + CHUNK, :]
        gc = g_ref[r0:r0 + CHUNK, :].astype(F32)
        for h in range(hg):
            ls = slice(h * HEAD_DIM, (h + 1) * HEAD_DIM)
            qp, kp, kl, vh = qp_all[:, ls], kp_all[:, ls], kl_all[:, ls], vc[:, ls]
            a = lax.dot_general(qp, kp, (((1,), (1,)), ((), ())), preferred_element_type=F32)
            a = jnp.where(causal, a, 0.0)
            st = st_ref[h]
            o = jnp.dot(a.astype(BF16), vh, preferred_element_type=F32)
            o = o + lax.dot_general(qp, st.astype(BF16), (((1,), (1,)), ((), ())),
                                    preferred_element_type=F32)
            vt = vh.astype(F32).T.astype(BF16)
            st_ref[h] = st * dl[:, ls] + jnp.dot(vt, kl, preferred_element_type=F32)
            o = o * lax.rsqrt(jnp.mean(o * o, axis=-1, keepdims=True) + RMS_EPS)
            gh = gc[:, ls]
            o = o * gain[:, ls] * (gh * _sigmoid(gh))
            o_ref[r0:r0 + CHUNK, ls] = o.astype(o_ref.dtype)


def _hgrn(qb, fb, ib, gb, lb, gain, rows, hg):
    T, W = qb.shape
    wb = hg * HEAD_DIM
    big = pl.BlockSpec((rows, wb), lambda h, i: (i, h))
    vec = pl.BlockSpec((1, wb), lambda h, i: (0, h))
    return pl.pallas_call(
        functools.partial(_hgrn_kernel, rows=rows, hg=hg),
        grid=(W // wb, T // rows),
        in_specs=[big, big, big, big, vec, vec],
        out_specs=big,
        out_shape=jax.ShapeDtypeStruct((T, W), BF16),
        scratch_shapes=[pltpu.VMEM((hg, HEAD_DIM, HEAD_DIM), F32)],
        compiler_params=_cparams(("parallel", "arbitrary")),
        name="hgrn",
    )(qb, fb, ib, gb, lb, gain)


def _merge_kernel(oa_ref, ob_ref, wa_ref, wb_ref, ga_ref, gb_ref, y_ref):
    a = jnp.dot(oa_ref[...], wa_ref[...], preferred_element_type=F32)
    b = jnp.dot(ob_ref[...], wb_ref[...], preferred_element_type=F32)
    y_ref[...] = (ga_ref[...].astype(F32) * a + gb_ref[...].astype(F32) * b).astype(y_ref.dtype)


def _merge(oa, ob, wa, wb, gates, tm, tn):
    T, KA = oa.shape
    KB = ob.shape[1]
    D = wa.shape[1]
    nd = D // tn
    return pl.pallas_call(
        _merge_kernel,
        grid=(T // tm, nd),
        in_specs=[pl.BlockSpec((tm, KA), lambda i, j: (i, 0)),
                  pl.BlockSpec((tm, KB), lambda i, j: (i, 0)),
                  pl.BlockSpec((KA, tn), lambda i, j: (0, j)),
                  pl.BlockSpec((KB, tn), lambda i, j: (0, j)),
                  pl.BlockSpec((tm, tn), lambda i, j: (i, j)),
                  pl.BlockSpec((tm, tn), lambda i, j: (i, j + nd))],
        out_specs=pl.BlockSpec((tm, tn), lambda i, j: (i, j)),
        out_shape=jax.ShapeDtypeStruct((T, D), BF16),
        compiler_params=_cparams(("parallel", "arbitrary")),
        name="merge",
    )(oa, ob, wa, wb, gates, gates)


def _resid_mm_kernel(y_ref, w_ref, x_ref, r_ref, *, alpha):
    z = jnp.dot(y_ref[...], w_ref[...], preferred_element_type=F32)
    r_ref[...] = alpha * x_ref[...] + z


def _resid_mm(y, w, x, alpha, tm, tn):
    T, K = y.shape
    D = w.shape[1]
    return pl.pallas_call(
        functools.partial(_resid_mm_kernel, alpha=alpha),
        grid=(T // tm, D // tn),
        in_specs=[pl.BlockSpec((tm, K), lambda i, j: (i, 0)),
                  pl.BlockSpec((K, tn), lambda i, j: (0, j)),
                  pl.BlockSpec((tm, tn), lambda i, j: (i, j))],
        out_specs=pl.BlockSpec((tm, tn), lambda i, j: (i, j)),
        out_shape=jax.ShapeDtypeStruct((T, D), F32),
        compiler_params=_cparams(("parallel", "arbitrary")),
        name="resid_mm",
    )(y, w, x)


def _ln_route_kernel(r_ref, g_ref, b_ref, wr_ref, br_ref, x1_ref, x1b_ref, lg_ref):
    x1 = _layer_norm_rows(r_ref[...], g_ref[...], b_ref[...])
    x1_ref[...] = x1
    x1b = x1.astype(BF16)
    x1b_ref[...] = x1b
    lg_ref[...] = jnp.dot(x1b, wr_ref[...], preferred_element_type=F32) + br_ref[...]


def _ln_route(r, g, b, wr, br, tm):
    T, D = r.shape
    row = pl.BlockSpec((tm, D), lambda i: (i, 0))
    vec = pl.BlockSpec((1, D), lambda i: (0, 0))
    return pl.pallas_call(
        _ln_route_kernel,
        grid=(T // tm,),
        in_specs=[row, vec, vec,
                  pl.BlockSpec((D, LANES), lambda i: (0, 0)),
                  pl.BlockSpec((1, LANES), lambda i: (0, 0))],
        out_specs=[row, row, pl.BlockSpec((tm, LANES), lambda i: (i, 0))],
        out_shape=[jax.ShapeDtypeStruct((T, D), F32),
                   jax.ShapeDtypeStruct((T, D), BF16),
                   jax.ShapeDtypeStruct((T, LANES), F32)],
        compiler_params=_cparams(("parallel",)),
        name="ln_route",
    )(r, g, b, wr, br)


def _route_kernel(lg_ref, ids_ref, wts_ref, cnt_ref, carry_ref, *, tm, n_groups, per_group):
    @pl.when(pl.program_id(0) == 0)
    def _():
        carry_ref[...] = jnp.zeros_like(carry_ref)

    lg = lg_ref[...]
    lane = lax.broadcasted_iota(jnp.int32, (tm, LANES), 1)

    def first_argmax(v):
        mx = jnp.max(v, axis=-1, keepdims=True)
        idx = jnp.min(jnp.where(v == mx, lane, LANES), axis=-1, keepdims=True)
        return mx, idx

    gl = jnp.where(lane < n_groups, lg, NEG_INF)
    gmax, gsel = first_argmax(gl)
    g_w = 1.0 / jnp.sum(jnp.exp(gl - gmax), axis=-1, keepdims=True)

    e_lane = lane - n_groups
    in_group = (e_lane >= gsel * per_group) & (e_lane < (gsel + 1) * per_group)
    el = jnp.where(in_group, lg, NEG_INF)
    m1, i1 = first_argmax(el)
    m2, i2 = first_argmax(jnp.where(lane == i1, NEG_INF, el))
    r21 = jnp.exp(m2 - m1)
    w1 = g_w / (1.0 + r21)
    w2 = g_w * r21 / (1.0 + r21)

    onehot = ((lane == i1) | (lane == i2)).astype(BF16)
    row = lax.broadcasted_iota(jnp.int32, (tm, tm), 0)
    col = lax.broadcasted_iota(jnp.int32, (tm, tm), 1)
    strict = (col < row).astype(BF16)
    prefix = jnp.dot(strict, onehot, preferred_element_type=F32) + carry_ref[...]
    rank1 = jnp.sum(jnp.where(lane == i1, prefix, 0.0), axis=-1, keepdims=True)
    rank2 = jnp.sum(jnp.where(lane == i2, prefix, 0.0), axis=-1, keepdims=True)
    ones = jnp.ones((8, tm), BF16)
    carry = carry_ref[...] + jnp.dot(ones, onehot, preferred_element_type=F32)[0:1]
    carry_ref[...] = carry
    cnt_ref[...] = carry

    e1 = i1 - n_groups
    e2 = i2 - n_groups
    ids = jnp.where(lane == 0, e1, jnp.where(lane == 1, e2, 0))
    ids = jnp.where(lane == 2, rank1.astype(jnp.int32), ids)
    ids = jnp.where(lane == 3, rank2.astype(jnp.int32), ids)
    ids_ref[...] = ids
    wts_ref[...] = jnp.where(lane == 0, w1, jnp.where(lane == 1, w2, 0.0))


def _route(logits, n_groups, per_group, tm):
    T = logits.shape[0]
    blk = pl.BlockSpec((tm, LANES), lambda i: (i, 0))
    return pl.pallas_call(
        functools.partial(_route_kernel, tm=tm, n_groups=n_groups, per_group=per_group),
        grid=(T // tm,),
        in_specs=[blk],
        out_specs=[blk, blk, pl.BlockSpec((1, LANES), lambda i: (0, 0))],
        out_shape=[jax.ShapeDtypeStruct((T, LANES), jnp.int32),
                   jax.ShapeDtypeStruct((T, LANES), F32),
                   jax.ShapeDtypeStruct((1, LANES), F32)],
        scratch_shapes=[pltpu.VMEM((1, LANES), F32)],
        compiler_params=_cparams(("arbitrary",)),
        name="route",
    )(logits)


def _experts_kernel(be_ref, cur_ref, nxt_ref, x_hbm, wg_ref, wu_ref, wd_ref, m_hbm,
                    xbuf, obuf, gsem, ssem, *, n_tok):
    del be_ref
    i = pl.program_id(0)
    nb = pl.num_programs(0)
    slot = lax.rem(i, 2)

    def start_gather(code_ref, s):
        def body(r, _):
            t = lax.shift_right_logical(code_ref[0, 0, r], 1)
            src = jnp.where(t < n_tok, t, 0)
            pltpu.make_async_copy(x_hbm.at[pl.ds(src, 1)], xbuf.at[s, pl.ds(r, 1)], gsem.at[s]).start()
            return 0
        lax.fori_loop(0, EXPERT_BLOCK, body, 0, unroll=8)

    def wait_gather(s):
        pltpu.make_async_copy(x_hbm.at[pl.ds(0, EXPERT_BLOCK)], xbuf.at[s], gsem.at[s]).wait()

    def wait_scatter(s):
        pltpu.make_async_copy(obuf.at[s], m_hbm.at[0, pl.ds(0, EXPERT_BLOCK)], ssem.at[s]).wait()

    @pl.when(i == 0)
    def _():
        start_gather(cur_ref, 0)
        obuf[...] = jnp.zeros_like(obuf)
        for pick in range(TOP_K):
            for s in range(2):
                spare = m_hbm.at[pick, pl.ds(n_tok + s * EXPERT_BLOCK, EXPERT_BLOCK)]
                zero_copy = pltpu.make_async_copy(obuf.at[s], spare, ssem.at[s])
                zero_copy.start()
                zero_copy.wait()

    @pl.when(i + 1 < nb)
    def _():
        start_gather(nxt_ref, 1 - slot)

    wait_gather(slot)

    @pl.when(i >= 2)
    def _():
        wait_scatter(slot)

    xb = xbuf[slot].astype(BF16)
    g = jnp.dot(xb, wg_ref[...], preferred_element_type=F32)
    u = jnp.dot(xb, wu_ref[...], preferred_element_type=F32)
    hid = (g * _sigmoid(g) * u).astype(BF16)
    obuf[slot] = jnp.dot(hid, wd_ref[...], preferred_element_type=F32)

    def scatter(r, _):
        code = cur_ref[0, 0, r]
        t = lax.shift_right_logical(code, 1)
        pick = code & 1
        pltpu.make_async_copy(obuf.at[slot, pl.ds(r, 1)], m_hbm.at[pick, pl.ds(t, 1)], ssem.at[slot]).start()
        return 0
    lax.fori_loop(0, EXPERT_BLOCK, scatter, 0, unroll=8)

    @pl.when(i == nb - 1)
    def _():
        wait_scatter(slot)
        wait_scatter(1 - slot)


def _experts(block_exp, codes, x1, wg, wu, wd):
    T, D = x1.shape
    P = codes.shape[0]
    DE = wg.shape[2]
    nb = P // EXPERT_BLOCK
    assert nb >= 2
    codes3 = codes.reshape(nb, 1, EXPERT_BLOCK)
    grid_spec = pltpu.PrefetchScalarGridSpec(
        num_scalar_prefetch=1,
        grid=(nb,),
        in_specs=[pl.BlockSpec((1, 1, EXPERT_BLOCK), lambda i, be: (i, 0, 0), memory_space=pltpu.SMEM),
                  pl.BlockSpec((1, 1, EXPERT_BLOCK), lambda i, be: (jnp.minimum(i + 1, nb - 1), 0, 0),
                               memory_space=pltpu.SMEM),
                  pl.BlockSpec(memory_space=pl.ANY),
                  pl.BlockSpec((None, D, DE), lambda i, be: (be[i], 0, 0)),
                  pl.BlockSpec((None, D, DE), lambda i, be: (be[i], 0, 0)),
                  pl.BlockSpec((None, DE, D), lambda i, be: (be[i], 0, 0))],
        out_specs=pl.BlockSpec(memory_space=pl.ANY),
        scratch_shapes=[pltpu.VMEM((2, EXPERT_BLOCK, D), F32),
                        pltpu.VMEM((2, EXPERT_BLOCK, D), F32),
                        pltpu.SemaphoreType.DMA((2,)),
                        pltpu.SemaphoreType.DMA((2,))],
    )
    return pl.pallas_call(
        functools.partial(_experts_kernel, n_tok=T),
        grid_spec=grid_spec,
        out_shape=jax.ShapeDtypeStruct((TOP_K, T + DUMP_ROWS, D), F32),
        compiler_params=_cparams(("arbitrary",)),
        name="experts",
    )(block_exp, codes3, codes3, x1, wg, wu, wd)


def _ln2_kernel(x1_ref, m0_ref, m1_ref, wts_ref, g_ref, b_ref, x2_ref, x2b_ref, *, alpha):
    w = wts_ref[...]
    r = alpha * x1_ref[...] + w[:, 0:1] * m0_ref[...] + w[:, 1:2] * m1_ref[...]
    x2 = _layer_norm_rows(r, g_ref[...], b_ref[...])
    x2_ref[...] = x2
    x2b_ref[...] = x2.astype(BF16)


def _ln2(x1, m, wts, g, b, alpha, tm):
    T, D = x1.shape
    row = pl.BlockSpec((tm, D), lambda i: (i, 0))
    vec = pl.BlockSpec((1, D), lambda i: (0, 0))
    return pl.pallas_call(
        functools.partial(_ln2_kernel, alpha=alpha),
        grid=(T // tm,),
        in_specs=[row, pl.BlockSpec((None, tm, D), lambda i: (0, i, 0)),
                  pl.BlockSpec((None, tm, D), lambda i: (1, i, 0)),
                  pl.BlockSpec((tm, LANES), lambda i: (i, 0)), vec, vec],
        out_specs=[row, row],
        out_shape=[jax.ShapeDtypeStruct((T, D), F32), jax.ShapeDtypeStruct((T, D), BF16)],
        compiler_params=_cparams(("parallel",)),
        name="ln2",
    )(x1, m, m, wts, g, b)


def _ple_kernel(x2b_ref, wpg_ref, bpg_ref, p_ref, wpe_ref, x2_ref, o_ref):
    z = jnp.dot(x2b_ref[...], wpg_ref[...], preferred_element_type=F32) + bpg_ref[...]
    e = jnp.dot(p_ref[...], wpe_ref[...], preferred_element_type=F32)
    o_ref[...] = x2_ref[...] + _sigmoid(z) * e


def _ple(x2b, wpg, bpg, pb, wpe, x2, tm, tn):
    T, D = x2b.shape
    PL = pb.shape[1]
    return pl.pallas_call(
        _ple_kernel,
        grid=(T // tm, D // tn),
        in_specs=[pl.BlockSpec((tm, D), lambda i, j: (i, 0)),
                  pl.BlockSpec((D, tn), lambda i, j: (0, j)),
                  pl.BlockSpec((1, tn), lambda i, j: (0, j)),
                  pl.BlockSpec((tm, PL), lambda i, j: (i, 0)),
                  pl.BlockSpec((PL, tn), lambda i, j: (0, j)),
                  pl.BlockSpec((tm, tn), lambda i, j: (i, j))],
        out_specs=pl.BlockSpec((tm, tn), lambda i, j: (i, j)),
        out_shape=jax.ShapeDtypeStruct((T, D), F32),
        compiler_params=_cparams(("parallel", "arbitrary")),
        name="ple",
    )(x2b, wpg, bpg, pb, wpe, x2)


def _tile(n, pref):
    return pref if n % pref == 0 else n


def _layer(x, p, w_in, b_fox_f, lb, hgrn_gain, w_a, w_b, w_o, ln1_g, ln1_b, w_rg, b_rg, w_re, b_re,
           w_eg, w_eu, w_ed, ln2_g, ln2_b, w_pg, b_pg, w_pe, alpha):
    T, D = x.shape
    n_fox = b_fox_f.shape[0]
    fw = n_fox * HEAD_DIM
    hw = lb.shape[0]
    n_groups = w_rg.shape[1]
    n_exp = w_re.shape[1]
    per_group = n_exp // n_groups
    assert n_groups + n_exp <= LANES

    tm = _tile(T, 1024)
    tn = _tile(D, 512)
    scale = HEAD_DIM ** -0.5

    xb = x.astype(BF16)
    f0 = 3 * fw
    h0 = f0 + n_fox
    w_main = jnp.concatenate([w_in[:, :f0], w_in[:, h0:]], axis=1).astype(BF16)
    w_f = jnp.zeros((D, LANES), BF16).at[:, :n_fox].set(w_in[:, f0:h0].astype(BF16))
    b_f = jnp.zeros((1, LANES), F32).at[0, :n_fox].set(b_fox_f)

    tnp = _tile(fw, 512)
    ident = lambda a: a
    q = _proj(xb, w_main, 0, fw, BF16, lambda a: a * (scale * LOG2E), tm, tnp)
    kv = _proj(xb, w_main, fw, 2 * fw, BF16, ident, tm, tnp)
    o_h = 3 * fw
    tnh = _tile(hw, 512)
    qh = _proj(xb, w_main, o_h, hw, BF16, lambda a: a * scale, tm, tnh)
    fh = _proj(xb, w_main, o_h + hw, hw, F32, ident, tm, tnh)
    ih = _proj(xb, w_main, o_h + 2 * hw, hw, BF16, ident, tm, tnh)
    gh = _proj(xb, w_main, o_h + 3 * hw, hw, BF16, ident, tm, tnh)
    gates = _proj(xb, w_main, o_h + 4 * hw, 2 * D, BF16, _sigmoid, tm, tn)

    c = _fox_bias(xb, w_f, b_f, _tile(T, 512))
    tk = _tile(T, 512)
    tq = _tile(T, 2 * tk)
    ck = c[:, :n_fox].T.reshape(n_fox, T // tk, 1, tk)
    oa = _fox_attn(q, kv, ck, n_fox, tq, tk)

    rows = _tile(T, 256)
    ob = _hgrn(qh, fh, ih, gh, lb.reshape(1, hw), hgrn_gain.reshape(1, hw), rows, 2)

    y = _merge(oa, ob, w_a.astype(BF16), w_b.astype(BF16), gates, tm, tn)
    r1 = _resid_mm(y, w_o.astype(BF16), x, alpha, tm, tn)
    w_r = jnp.zeros((D, LANES), BF16).at[:, :n_groups + n_exp].set(
        jnp.concatenate([w_rg, w_re], axis=1).astype(BF16))
    b_r = jnp.zeros((1, LANES), F32).at[0, :n_groups + n_exp].set(jnp.concatenate([b_rg, b_re]))
    x1, x1b, logits = _ln_route(r1, ln1_g.reshape(1, D), ln1_b.reshape(1, D), w_r, b_r, _tile(T, 256))
    del x1b

    ids, wts, counts = _route(logits, n_groups, per_group, _tile(T, 512))
    expert_id = ids[:, 0:2]
    rank = ids[:, 2:4]
    cnt = counts[0, n_groups:n_groups + n_exp].astype(jnp.int32)
    pcnt = (cnt + EXPERT_BLOCK - 1) // EXPERT_BLOCK * EXPERT_BLOCK
    pends = jnp.cumsum(pcnt)
    pstarts = pends - pcnt
    M = T * TOP_K
    n_blocks = -(-(M + n_exp * (EXPERT_BLOCK - 1)) // EXPERT_BLOCK)
    P = n_blocks * EXPERT_BLOCK
    dest = pstarts[expert_id] + rank
    tok_code = 2 * jnp.arange(T, dtype=jnp.int32)[:, None] + jnp.arange(TOP_K, dtype=jnp.int32)[None, :]
    empty_code = 2 * (T + jnp.arange(P, dtype=jnp.int32) % DUMP_ROWS)
    codes = empty_code.at[dest.reshape(M)].set(tok_code.reshape(M))
    block_start = jnp.arange(n_blocks, dtype=jnp.int32) * EXPERT_BLOCK
    block_exp = jnp.minimum(jnp.sum(pends[None, :] <= block_start[:, None], axis=1), n_exp - 1).astype(jnp.int32)

    m = _experts(block_exp, codes, x1, w_eg.astype(BF16), w_eu.astype(BF16), w_ed.astype(BF16))

    x2, x2b = _ln2(x1, m, wts, ln2_g.reshape(1, D), ln2_b.reshape(1, D), alpha, _tile(T, 256))
    return _ple(x2b, w_pg.astype(BF16), b_pg.reshape(1, D), p.astype(BF16), w_pe.astype(BF16), x2, tm, tn)


def kernel(x, p, w_in, b_fox_f, hgrn_lb, hgrn_norm_g, w_branch_a, w_branch_b, w_out, ln1_g, ln1_b, w_group_router, b_group_router, w_expert_router, b_expert_router, w_exp_gate, w_exp_up, w_exp_down, ln2_g, ln2_b, w_ple_gate, b_ple_gate, w_ple_proj):
    B, T, D = x.shape
    depth = w_in.shape[0]
    alpha = (2 * depth) ** 0.25
    lower_bounds = jnp.cumsum(jax.nn.softmax(hgrn_lb.astype(F32), axis=0), axis=0)
    outs = []
    for bi in range(B):
        xc = x[bi]
        for l in range(depth):
            xc = _layer(xc, p[l, bi], w_in[l], b_fox_f[l], lower_bounds[l], hgrn_norm_g[l],
                        w_branch_a[l], w_branch_b[l], w_out[l], ln1_g[l], ln1_b[l],
                        w_group_router[l], b_group_router[l], w_expert_router[l], b_expert_router[l],
                        w_exp_gate[l], w_exp_up[l], w_exp_down[l], ln2_g[l], ln2_b[l],
                        w_ple_gate[l], b_ple_gate[l], w_ple_proj[l], alpha)
        outs.append(xc)
    return jnp.stack(outs, axis=0)
```
